```python
import math
import jax, jax.numpy as jnp
from jax import lax
import numpy as np

D_MODEL = 1024
BATCH = 16
SEQ = 2048
DEPTH = 2

NSA_DH = 64
NSA_HEADS = (D_MODEL // 2) // NSA_DH
NSA_KV_GROUPS = 2
NSA_HPG = NSA_HEADS // NSA_KV_GROUPS
NSA_WIDTH = NSA_HEADS * NSA_DH
NSA_KV_WIDTH = NSA_KV_GROUPS * NSA_DH
CMP_LEN = 32
CMP_STRIDE = 16
CMP_HIDDEN = 2 * NSA_DH
SLC_BLOCK = 64
N_SELECT = 16
WINDOW = 512
Q_BLOCK = 64
ROPE_THETA = 500000.0
ROPE_DIMS = NSA_DH // 4
MLSTM_HEADS = 4
MLSTM_DH = (D_MODEL - NSA_WIDTH) // MLSTM_HEADS
MLSTM_WIDTH = MLSTM_HEADS * MLSTM_DH
MLSTM_CHUNK = 64
MLSTM_CONV = 4
D_MIX = NSA_WIDTH + MLSTM_WIDTH
D_FF = ((8 * D_MODEL // 3 + 127) // 128) * 128
EPS = 1e-6
NEG_INF = -1e30
FORCE_SCORE = 1e9
IN_SIZES = [NSA_WIDTH] + [NSA_KV_WIDTH] * 6 + [3 * NSA_HEADS] + [MLSTM_WIDTH] * 4 + [MLSTM_HEADS] * 2
IN_COLS = sum(IN_SIZES)
IN_SPLITS = [int(s) for s in np.cumsum(IN_SIZES)[:-1]]

kernel_name = 'hybrid_nsa_mlstm_macaron_sandwich'


def rms_norm(x, g):
    x32 = x.astype(jnp.float32)
    y = x32 * lax.rsqrt(jnp.mean(x32 * x32, axis=-1, keepdims=True) + EPS)
    return (y * g.astype(jnp.float32)).astype(x.dtype)


def swiglu(x, w_gu, w_down):
    gate, up = jnp.split(x @ w_gu, 2, axis=-1)
    return (jax.nn.silu(gate) * up) @ w_down


def rope_tables(seq):
    pos = jnp.arange(seq, dtype=jnp.float32)
    inv_freq = ROPE_THETA ** (-jnp.arange(0, ROPE_DIMS, 2, dtype=jnp.float32) / ROPE_DIMS)
    ang = pos[:, None] * inv_freq[None, :]
    return jnp.cos(ang), jnp.sin(ang)


def partial_rope(x, cos, sin):
    xr, xp = x[..., :ROPE_DIMS], x[..., ROPE_DIMS:]
    x1, x2 = jnp.split(xr, 2, axis=-1)
    c, s = cos[None, :, None, :], sin[None, :, None, :]
    rot = jnp.concatenate([x1 * c - x2 * s, x2 * c + x1 * s], axis=-1).astype(x.dtype)
    return jnp.concatenate([rot, xp], axis=-1)


def compress_blocks(tok, pe, w1, w2):
    b, s = tok.shape[:2]
    n_cmp = (s - CMP_LEN) // CMP_STRIDE + 1
    idx = np.arange(n_cmp)[:, None] * CMP_STRIDE + np.arange(CMP_LEN)[None, :]
    blocks = tok[:, idx] + pe[None, None, :, None, :].astype(tok.dtype)
    flat = blocks.transpose(0, 3, 1, 2, 4).reshape(b, NSA_KV_GROUPS, n_cmp, CMP_LEN * NSA_DH)
    return jax.nn.gelu(flat @ w1) @ w2


def nsa_group(q, k_cmp, v_cmp, k_slc, v_slc, k_win, v_win, gates, pe, w1, w2, cos, sin):
    b, s = q.shape[:2]
    dt = q.dtype
    q = partial_rope(q.reshape(b, s, NSA_HEADS, NSA_DH), cos, sin)
    q = q.reshape(b, s, NSA_KV_GROUPS, NSA_HPG, NSA_DH)
    kv_shape = (b, s, NSA_KV_GROUPS, NSA_DH)
    k_slc = partial_rope(k_slc.reshape(kv_shape), cos, sin)
    k_win = partial_rope(k_win.reshape(kv_shape), cos, sin)
    v_slc = v_slc.reshape(kv_shape)
    v_win = v_win.reshape(kv_shape)
    kc = compress_blocks(k_cmp.reshape(kv_shape), pe[0], w1[0], w2[0])
    vc = compress_blocks(v_cmp.reshape(kv_shape), pe[1], w1[1], w2[1])
    n_cmp = kc.shape[2]
    cmp_end = jnp.arange(n_cmp) * CMP_STRIDE + CMP_LEN - 1
    n_slc = s // SLC_BLOCK
    n_sel = min(N_SELECT, n_slc)
    ci = np.arange(n_cmp)[:, None] * CMP_STRIDE
    sj = np.arange(n_slc)[None, :] * SLC_BLOCK
    overlap = jnp.asarray(((ci < sj + SLC_BLOCK) & (ci + CMP_LEN > sj)).astype(np.float32))
    ks_blk = k_slc.reshape(b, n_slc, SLC_BLOCK, NSA_KV_GROUPS, NSA_DH).transpose(0, 3, 1, 2, 4)
    vs_blk = v_slc.reshape(b, n_slc, SLC_BLOCK, NSA_KV_GROUPS, NSA_DH).transpose(0, 3, 1, 2, 4)
    kw_pad = jnp.pad(k_win, ((0, 0), (WINDOW, 0), (0, 0), (0, 0)))
    vw_pad = jnp.pad(v_win, ((0, 0), (WINDOW, 0), (0, 0), (0, 0)))
    scale = NSA_DH ** -0.5
    n_qb = s // Q_BLOCK
    q_blocks = jnp.moveaxis(q.reshape(b, n_qb, Q_BLOCK, NSA_KV_GROUPS, NSA_HPG, NSA_DH), 1, 0)
    b_ix = jnp.arange(b)[:, None, None, None]
    g_ix = jnp.arange(NSA_KV_GROUPS)[None, :, None, None]
    blk = jnp.arange(n_slc)
    r_off = jnp.arange(SLC_BLOCK)

    def block_fn(args):
        c, qb = args
        t = c * Q_BLOCK + jnp.arange(Q_BLOCK)
        s_c = jnp.einsum('bqghd,bgnd->bghqn', qb, kc).astype(jnp.float32) * scale
        valid_c = cmp_end[None, :] <= t[:, None]
        p_c = jax.nn.softmax(jnp.where(valid_c, s_c, NEG_INF), axis=-1)
        p_c = p_c * jnp.any(valid_c, axis=-1)[:, None].astype(jnp.float32)
        o_c = jnp.einsum('bghqn,bgnd->bqghd', p_c.astype(dt), vc)
        imp = jnp.einsum('bghqn,nj->bgqj', p_c, overlap)
        cur = t // SLC_BLOCK
        forced = (blk[None, :] == 0) | (blk[None, :] == cur[:, None]) | (blk[None, :] == cur[:, None] - 1)
        blk_valid = blk[None, :] * SLC_BLOCK <= t[:, None]
        imp = jnp.where(forced, FORCE_SCORE, imp)
        imp = jnp.where(blk_valid, imp, NEG_INF)
        _, idx = lax.top_k(imp, n_sel)
        k_sel = ks_blk[b_ix, g_ix, idx]
        v_sel = vs_blk[b_ix, g_ix, idx]
        tok_pos = idx[..., None] * SLC_BLOCK + r_off
        valid_s = tok_pos <= t[None, None, :, None, None]
        s_s = jnp.einsum('bqghd,bgqnrd->bghqnr', qb, k_sel).astype(jnp.float32) * scale
        s_s = jnp.where(valid_s[:, :, None], s_s, NEG_INF)
        p_s = jax.nn.softmax(s_s.reshape(s_s.shape[:4] + (-1,)), axis=-1).reshape(s_s.shape)
        o_s = jnp.einsum('bghqnr,bgqnrd->bqghd', p_s.astype(dt), v_sel)
        start = c * Q_BLOCK
        k_w = lax.dynamic_slice_in_dim(kw_pad, start, Q_BLOCK + WINDOW, axis=1)
        v_w = lax.dynamic_slice_in_dim(vw_pad, start, Q_BLOCK + WINDOW, axis=1)
        kpos = start - WINDOW + jnp.arange(Q_BLOCK + WINDOW)
        valid_w = (kpos[None, :] <= t[:, None]) & (kpos[None, :] > t[:, None] - WINDOW) & (kpos[None, :] >= 0)
        s_w = jnp.einsum('bqghd,bkgd->bghqk', qb, k_w).astype(jnp.float32) * scale
        p_w = jax.nn.softmax(jnp.where(valid_w, s_w, NEG_INF), axis=-1)
        o_w = jnp.einsum('bghqk,bkgd->bqghd', p_w.astype(dt), v_w)
        return o_c, o_s, o_w

    o_c, o_s, o_w = lax.map(block_fn, (jnp.arange(n_qb), q_blocks))
    unblock = lambda o: jnp.moveaxis(o, 0, 1).reshape(b, s, NSA_HEADS, NSA_DH).astype(jnp.float32)
    g = jax.nn.sigmoid(gates.astype(jnp.float32)).reshape(b, s, 3, NSA_HEADS, 1)
    out = g[:, :, 0] * unblock(o_c) + g[:, :, 1] * unblock(o_s) + g[:, :, 2] * unblock(o_w)
    return out.reshape(b, s, NSA_WIDTH).astype(dt)


def mlstm_group(q, k, v, o_pre, i_pre, f_pre, conv_w, conv_b, norm_g, i_bias, f_bias):
    b, s = q.shape[:2]
    dt = q.dtype
    qk = jnp.concatenate([q, k], axis=-1)
    ch = qk.shape[-1]
    qk = lax.conv_general_dilated(qk, conv_w.astype(qk.dtype)[:, None, :], window_strides=(1,),
                                  padding=[(MLSTM_CONV - 1, 0)], dimension_numbers=('NWC', 'WIO', 'NWC'),
                                  feature_group_count=ch) + conv_b.astype(qk.dtype)
    q, k = jnp.split(jax.nn.silu(qk), 2, axis=-1)
    heads = lambda a: a.reshape(b, s, MLSTM_HEADS, MLSTM_DH).transpose(0, 2, 1, 3).astype(jnp.float32)
    q, k, v = heads(q), heads(k) * (MLSTM_DH ** -0.5), heads(v)
    i_log = (i_pre.astype(jnp.float32) + i_bias.astype(jnp.float32)).transpose(0, 2, 1)
    f_log = jax.nn.log_sigmoid(f_pre.astype(jnp.float32) + f_bias.astype(jnp.float32)).transpose(0, 2, 1)
    L = MLSTM_CHUNK
    nc = s // L
    chunk = lambda a: jnp.moveaxis(a.reshape(a.shape[:2] + (nc, L) + a.shape[3:]), 2, 0)
    xs = (chunk(q), chunk(k), chunk(v), chunk(i_log), chunk(f_log))
    tril = jnp.tril(jnp.ones((L, L), dtype=bool))

    def step(carry, inp):
        C, n, m = carry
        qc, kc, vc, ic, fc = inp
        bcum = jnp.cumsum(fc, axis=-1)
        D = jnp.where(tril, bcum[..., :, None] - bcum[..., None, :] + ic[..., None, :], NEG_INF)
        inter = bcum + m[..., None]
        m_t = jnp.maximum(jnp.max(D, axis=-1), inter)
        w_in = jnp.exp(D - m_t[..., None])
        w_prev = jnp.exp(inter - m_t)
        sc = jnp.einsum('bhtd,bhsd->bhts', qc, kc) * w_in
        num = jnp.einsum('bhts,bhsd->bhtd', sc, vc) + w_prev[..., None] * jnp.einsum('bhtd,bhde->bhte', qc, C)
        den = jnp.sum(sc, axis=-1) + w_prev * jnp.einsum('bhtd,bhd->bht', qc, n)
        h = num / jnp.maximum(jnp.abs(den), jnp.exp(-m_t))[..., None]
        b_last = bcum[..., -1]
        dec = b_last[..., None] - bcum + ic
        m_new = jnp.maximum(b_last + m, jnp.max(dec, axis=-1))
        wk = jnp.exp(dec - m_new[..., None])
        carry_scale = jnp.exp(b_last + m - m_new)
        C_new = carry_scale[..., None, None] * C + jnp.einsum('bhs,bhsd,bhse->bhde', wk, kc, vc)
        n_new = carry_scale[..., None] * n + jnp.einsum('bhs,bhsd->bhd', wk, kc)
        return (C_new, n_new, m_new), h

    init = (jnp.zeros((b, MLSTM_HEADS, MLSTM_DH, MLSTM_DH), jnp.float32),
            jnp.zeros((b, MLSTM_HEADS, MLSTM_DH), jnp.float32),
            jnp.zeros((b, MLSTM_HEADS), jnp.float32))
    _, hs = lax.scan(step, init, xs)
    h = jnp.moveaxis(hs, 0, 2).reshape(b, MLSTM_HEADS, s, MLSTM_DH).transpose(0, 2, 1, 3)
    h = h * lax.rsqrt(jnp.mean(h * h, axis=-1, keepdims=True) + EPS)
    h = h * norm_g.astype(jnp.float32).reshape(MLSTM_HEADS, MLSTM_DH)
    o = jax.nn.sigmoid(o_pre.astype(jnp.float32)).reshape(b, s, MLSTM_HEADS, MLSTM_DH)
    return (h * o).reshape(b, s, MLSTM_WIDTH).astype(dt)


def setup_inputs(seed: int = 0) -> dict:
    key = jax.random.key(seed)
    ks = jax.random.split(key, 16)
    nrm = lambda k, shape, sc: jax.random.normal(k, shape, jnp.float32) * sc
    return {
        'x': nrm(ks[0], (BATCH, SEQ, D_MODEL), 1.0),
        'norm_g': 1.0 + nrm(ks[1], (DEPTH, 6, D_MODEL), 0.02),
        'ffn1_w_gu': nrm(ks[2], (DEPTH, D_MODEL, 2 * D_FF), D_MODEL ** -0.5),
        'ffn1_w_down': nrm(ks[3], (DEPTH, D_FF, D_MODEL), D_FF ** -0.5),
        'ffn2_w_gu': nrm(ks[4], (DEPTH, D_MODEL, 2 * D_FF), D_MODEL ** -0.5),
        'ffn2_w_down': nrm(ks[5], (DEPTH, D_FF, D_MODEL), D_FF ** -0.5),
        'mix_w_in': nrm(ks[6], (DEPTH, D_MODEL, IN_COLS), D_MODEL ** -0.5),
        'mix_w_out': nrm(ks[7], (DEPTH, D_MIX, D_MODEL), D_MIX ** -0.5),
        'nsa_cmp_pe': nrm(ks[8], (DEPTH, 2, CMP_LEN, NSA_DH), 0.02),
        'nsa_cmp_w1': nrm(ks[9], (DEPTH, 2, CMP_LEN * NSA_DH, CMP_HIDDEN), (CMP_LEN * NSA_DH) ** -0.5),
        'nsa_cmp_w2': nrm(ks[10], (DEPTH, 2, CMP_HIDDEN, NSA_DH), CMP_HIDDEN ** -0.5),
        'mlstm_conv_w': nrm(ks[11], (DEPTH, MLSTM_CONV, 2 * MLSTM_WIDTH), MLSTM_CONV ** -0.5),
        'mlstm_conv_b': nrm(ks[12], (DEPTH, 2 * MLSTM_WIDTH), 0.02),
        'mlstm_i_bias': nrm(ks[13], (DEPTH, MLSTM_HEADS), 0.1),
        'mlstm_f_bias': jnp.linspace(3.0, 6.0, MLSTM_HEADS, dtype=jnp.float32)[None, :] + nrm(ks[14], (DEPTH, MLSTM_HEADS), 0.1),
        'mlstm_norm_g': 1.0 + nrm(ks[15], (DEPTH, MLSTM_WIDTH), 0.02),
    }


def reference(x, norm_g, ffn1_w_gu, ffn1_w_down, ffn2_w_gu, ffn2_w_down, mix_w_in, mix_w_out,
              nsa_cmp_pe, nsa_cmp_w1, nsa_cmp_w2, mlstm_conv_w, mlstm_conv_b, mlstm_i_bias,
              mlstm_f_bias, mlstm_norm_g):
    cos, sin = rope_tables(x.shape[1])
    for l in range(DEPTH):
        g = norm_g[l]
        h = swiglu(rms_norm(x, g[0]), ffn1_w_gu[l], ffn1_w_down[l])
        x = x + 0.5 * rms_norm(h, g[1])
        u = rms_norm(x, g[2])
        (nq, kcm, vcm, ksl, vsl, kwn, vwn, ngate,
         mq, mk, mv, mo, mi, mf) = jnp.split(u @ mix_w_in[l], IN_SPLITS, axis=-1)
        y_nsa = nsa_group(nq, kcm, vcm, ksl, vsl, kwn, vwn, ngate,
                          nsa_cmp_pe[l], nsa_cmp_w1[l], nsa_cmp_w2[l], cos, sin)
        y_mlstm = mlstm_group(mq, mk, mv, mo, mi, mf, mlstm_conv_w[l], mlstm_conv_b[l],
                              mlstm_norm_g[l], mlstm_i_bias[l], mlstm_f_bias[l])
        h = jnp.concatenate([y_nsa, y_mlstm], axis=-1) @ mix_w_out[l]
        x = x + rms_norm(h, g[3])
        h = swiglu(rms_norm(x, g[4]), ffn2_w_gu[l], ffn2_w_down[l])
        x = x + 0.5 * rms_norm(h, g[5])
    return x
```

```python
import functools

import numpy as np
import jax
import jax.numpy as jnp
from jax import lax
from jax.experimental import pallas as pl
from jax.experimental.pallas import tpu as pltpu

D_MODEL = 1024
NSA_DH = 64
NSA_HEADS = 8
NSA_GROUPS = 2
NSA_WIDTH = 512
CMP_LEN = 32
CMP_STRIDE = 16
CMP_HIDDEN = 128
SLC_BLOCK = 64
N_SELECT = 16
WINDOW = 512
ROPE_THETA = 500000.0
ROPE_DIMS = 16
MLSTM_HEADS = 4
MLSTM_DH = 128
MLSTM_WIDTH = 512
MLSTM_CHUNK = 64
MLSTM_CONV = 4
D_FF = 2816
EPS = 1e-6
NEG_INF = -1e30
FORCE_SCORE = 1e9
M_INIT = -1e38

LANES = 128
VMEM_LIMIT = 56 * 1024 * 1024

FFN_TM = 512
FF_CHUNK = 256
PROJ_TM = 512
ATT_TQ = 256
ATT_KC = 256

F32 = jnp.float32
BF16 = jnp.bfloat16

C_Q = 0
C_KC = 512
C_VC = 640
C_KS = 768
C_VS = 896
C_KW = 1024
C_VW = 1152
C_M = 1280
C_SMALL = 3328
PROJ_COLS = 3456
SM_I = 24
SM_F = 28


def _dot(a, b):
    return jnp.dot(a, b, preferred_element_type=F32)


def _dot_nt(a, b):
    return lax.dot_general(a, b, (((1,), (1,)), ((), ())), preferred_element_type=F32)


def _dot_tn(a, b):
    return lax.dot_general(a, b, (((0,), (0,)), ((), ())), preferred_element_type=F32)


def _dot_exact(a, b):
    return jnp.dot(a, b, preferred_element_type=F32, precision=lax.Precision.HIGHEST)


def _rms(x, g):
    return x * lax.rsqrt(jnp.mean(x * x, axis=-1, keepdims=True) + EPS) * g


def _params(sem):
    return pltpu.CompilerParams(dimension_semantics=sem, vmem_limit_bytes=VMEM_LIMIT)


def _ffn_body(x_ref, gpre_ref, gpost_ref, wgu_ref, wd_ref, o_ref, a_ref):
    x = x_ref[...]
    h = _rms(x, gpre_ref[...]).astype(BF16)
    for c in range(D_FF // FF_CHUNK):
        lo = c * FF_CHUNK
        g = _dot(h, wgu_ref[:, lo:lo + FF_CHUNK])
        u = _dot(h, wgu_ref[:, D_FF + lo:D_FF + lo + FF_CHUNK])
        a_ref[:, lo:lo + FF_CHUNK] = (g * jax.nn.sigmoid(g) * u).astype(BF16)
    y = _dot(a_ref[...], wd_ref[...])
    o_ref[...] = x + 0.5 * _rms(y, gpost_ref[...])


def _ffn(x, g_pre, g_post, w_gu, w_down):
    t = x.shape[0]
    return pl.pallas_call(
        _ffn_body,
        grid=(t // FFN_TM,),
        in_specs=[
            pl.BlockSpec((FFN_TM, D_MODEL), lambda i: (i, 0)),
            pl.BlockSpec((1, D_MODEL), lambda i: (0, 0)),
            pl.BlockSpec((1, D_MODEL), lambda i: (0, 0)),
            pl.BlockSpec((D_MODEL, 2 * D_FF), lambda i: (0, 0), pipeline_mode=pl.Buffered(1)),
            pl.BlockSpec((D_FF, D_MODEL), lambda i: (0, 0), pipeline_mode=pl.Buffered(1)),
        ],
        out_specs=pl.BlockSpec((FFN_TM, D_MODEL), lambda i: (i, 0)),
        out_shape=jax.ShapeDtypeStruct((t, D_MODEL), F32),
        scratch_shapes=[pltpu.VMEM((FFN_TM, D_FF), BF16)],
        compiler_params=_params(("parallel",)),
        name="ffn",
    )(x, g_pre, g_post, w_gu, w_down)


def _inproj_body(x_ref, g_ref, w_ref, cos_ref, sup_ref, sdn_ref,
                 q_ref, kc_ref, vc_ref, kv_ref, m_ref, sm_ref):
    u = _rms(x_ref[...], g_ref[...]).astype(BF16)

    def proj(lo, width=LANES):
        return _dot(u, w_ref[:, lo:lo + width])

    def rope(y):
        return (y * cos_ref[...] + pltpu.roll(y, LANES - ROPE_DIMS // 2, 1) * sup_ref[...]
                + pltpu.roll(y, ROPE_DIMS // 2, 1) * sdn_ref[...])

    scale = NSA_DH ** -0.5
    for p in range(NSA_WIDTH // LANES):
        q_ref[:, p * LANES:(p + 1) * LANES] = (rope(proj(C_Q + p * LANES)) * scale).astype(BF16)
    kc_ref[...] = proj(C_KC)
    vc_ref[...] = proj(C_VC)
    kv_ref[:, 0 * LANES:1 * LANES] = rope(proj(C_KS))
    kv_ref[:, 1 * LANES:2 * LANES] = proj(C_VS)
    kv_ref[:, 2 * LANES:3 * LANES] = rope(proj(C_KW))
    kv_ref[:, 3 * LANES:4 * LANES] = proj(C_VW)
    for p in range(4):
        m_ref[:, p * MLSTM_WIDTH:(p + 1) * MLSTM_WIDTH] = proj(C_M + p * MLSTM_WIDTH, MLSTM_WIDTH)
    sm_ref[...] = proj(C_SMALL)


def _inproj(x, g, w, cos_t, sup_t, sdn_t, seq):
    t = x.shape[0]
    nblk = seq // PROJ_TM
    row = lambda i: (i, 0)
    pos = lambda i: (i % nblk, 0)
    const = lambda i: (0, 0)
    return pl.pallas_call(
        _inproj_body,
        grid=(t // PROJ_TM,),
        in_specs=[
            pl.BlockSpec((PROJ_TM, D_MODEL), row),
            pl.BlockSpec((1, D_MODEL), const),
            pl.BlockSpec((D_MODEL, PROJ_COLS), const, pipeline_mode=pl.Buffered(1)),
            pl.BlockSpec((PROJ_TM, LANES), pos),
            pl.BlockSpec((PROJ_TM, LANES), pos),
            pl.BlockSpec((PROJ_TM, LANES), pos),
        ],
        out_specs=[
            pl.BlockSpec((PROJ_TM, NSA_WIDTH), row),
            pl.BlockSpec((PROJ_TM, LANES), row),
            pl.BlockSpec((PROJ_TM, LANES), row),
            pl.BlockSpec((PROJ_TM, 4 * LANES), row),
            pl.BlockSpec((PROJ_TM, 4 * MLSTM_WIDTH), row),
            pl.BlockSpec((PROJ_TM, LANES), row),
        ],
        out_shape=[
            jax.ShapeDtypeStruct((t, NSA_WIDTH), BF16),
            jax.ShapeDtypeStruct((t, LANES), F32),
            jax.ShapeDtypeStruct((t, LANES), F32),
            jax.ShapeDtypeStruct((t, 4 * LANES), F32),
            jax.ShapeDtypeStruct((t, 4 * MLSTM_WIDTH), F32),
            jax.ShapeDtypeStruct((t, LANES), F32),
        ],
        compiler_params=_params(("parallel",)),
        name="inproj",
    )(x, g, w, cos_t, sup_t, sdn_t)


def _pair_blockdiag(x, first):
    lane = lax.broadcasted_iota(jnp.int32, x.shape, 1)
    lo = lane < NSA_DH
    r = pltpu.roll(x, NSA_DH, 1)
    zero = jnp.zeros_like(x)
    if first is True:
        return jnp.where(lo, x, zero), jnp.where(lo, zero, r)
    if first is False:
        return jnp.where(lo, r, zero), jnp.where(lo, zero, x)
    return (jnp.where(lo, jnp.where(first, x, r), zero),
            jnp.where(lo, zero, jnp.where(first, r, x)))


def _compress_body(tk_ref, tv_ref, wa_ref, wb_ref, w2_ref, pea_ref, peb_ref,
                   ko_ref, vo_ref, sh_ref):
    n = tk_ref.shape[1]
    for which, (tok_ref, out_ref) in enumerate(((tk_ref, ko_ref), (tv_ref, vo_ref))):
        tok = tok_ref[0].astype(BF16)
        a = _dot(tok, wa_ref[which])
        b = _dot(tok, wb_ref[which])
        pe = _dot(pea_ref[which], wa_ref[which]) + _dot(peb_ref[which], wb_ref[which])
        sh_ref[0:n, :] = b
        sh_ref[n:n + 8, :] = jnp.zeros((8, sh_ref.shape[1]), F32)
        pre = a + sh_ref[pl.ds(1, n), :] + pe[0:1, :]
        hid = jax.nn.gelu(pre)
        out = _dot(hid.astype(BF16), w2_ref[which])
        for g in range(NSA_GROUPS):
            top, bot = _pair_blockdiag(out, g == 0)
            out_ref[0, g, 0:n, :] = top.astype(BF16)
            out_ref[0, g, n:2 * n, :] = bot.astype(BF16)


def _compress(tk, tv, wa, wb, w2, pea, peb):
    b, n, width = tk.shape
    hid2 = wa.shape[-1]
    tok_spec = pl.BlockSpec((1, n, width), lambda i: (i, 0, 0))
    full = lambda a: pl.BlockSpec(a.shape, lambda i: (0,) * a.ndim)
    out_spec = pl.BlockSpec((1, NSA_GROUPS, 2 * n, LANES), lambda i: (i, 0, 0, 0))
    out_sds = jax.ShapeDtypeStruct((b, NSA_GROUPS, 2 * n, LANES), BF16)
    return pl.pallas_call(
        _compress_body,
        grid=(b,),
        in_specs=[tok_spec, tok_spec, full(wa), full(wb), full(w2), full(pea), full(peb)],
        out_specs=[out_spec, out_spec],
        out_shape=[out_sds, out_sds],
        scratch_shapes=[pltpu.VMEM((n + 8, hid2), F32)],
        compiler_params=_params(("parallel",)),
        name="compress",
    )(tk, tv, wa, wb, w2, pea, peb)


def _cmpattn_body(q_ref, kp_ref, vp_ref, ovt_ref, oc_ref, qa_ref, *, n_cmp, n_slc):
    tq = q_ref.shape[0]
    qi = pl.program_id(1)
    t_col = qi * tq + lax.broadcasted_iota(jnp.int32, (tq, n_cmp), 0)
    n_lane = lax.broadcasted_iota(jnp.int32, (tq, n_cmp), 1)
    valid = (n_lane * CMP_STRIDE + (CMP_LEN - 1)) <= t_col

    blk = lax.broadcasted_iota(jnp.int32, (n_slc, tq), 0)
    t_row = qi * tq + lax.broadcasted_iota(jnp.int32, (n_slc, tq), 1)
    cur = t_row // SLC_BLOCK
    forced = (blk == 0) | (blk == cur) | (blk == cur - 1)
    blk_valid = blk * SLC_BLOCK <= t_row

    def softmax_half(s):
        s = jnp.where(valid, s, NEG_INF)
        m = jnp.max(s, axis=-1, keepdims=True)
        e = jnp.where(valid, jnp.exp(s - m), 0.0)
        l = jnp.sum(e, axis=-1, keepdims=True)
        return e * jnp.where(l > 0.0, 1.0 / l, 0.0)

    for g in range(NSA_GROUPS):
        psum = jnp.zeros((tq, n_cmp), F32)
        for j in range(2):
            p = 2 * g + j
            q2 = q_ref[:, p * LANES:(p + 1) * LANES]
            s2 = _dot_nt(q2, kp_ref[0, g])
            pa = softmax_half(s2[:, :n_cmp])
            pb = softmax_half(s2[:, n_cmp:])
            p2 = jnp.concatenate([pa, pb], axis=-1).astype(BF16)
            oc_ref[:, p * LANES:(p + 1) * LANES] = _dot(p2, vp_ref[0, g])
            psum = psum + pa + pb
        imp = lax.dot_general(ovt_ref[...], psum, (((1,), (1,)), ((), ())),
                              preferred_element_type=F32, precision=lax.Precision.HIGHEST)[0:n_slc]
        imp = jnp.where(forced, FORCE_SCORE, imp)
        imp = jnp.where(blk_valid, imp, NEG_INF)
        cnt = jnp.zeros((n_slc, tq), jnp.int32)
        for jp in range(n_slc):
            row = imp[jp:jp + 1, :]
            tie = (blk > jp).astype(jnp.int32)
            cnt = cnt + jnp.where(row > imp, 1, jnp.where(row == imp, tie, 0))
        selb = jnp.where(cnt < N_SELECT, 0.0, NEG_INF)
        selb = jnp.concatenate([selb, jnp.zeros((LANES - n_slc, tq), F32)], axis=0)
        selt = selb.T.astype(BF16)
        for j in range(2):
            p = 2 * g + j
            qa_ref[:, 2 * p * LANES:(2 * p + 1) * LANES] = q_ref[:, p * LANES:(p + 1) * LANES]
            qa_ref[:, (2 * p + 1) * LANES:(2 * p + 2) * LANES] = selt


def _cmpattn(q, kp, vp, ovt, batch, seq):
    t = q.shape[0]
    nq = seq // ATT_TQ
    n_cmp = kp.shape[2] // 2
    row = lambda b, i: (b * nq + i, 0)
    per_b = lambda b, i: (b, 0, 0, 0)
    return pl.pallas_call(
        functools.partial(_cmpattn_body, n_cmp=n_cmp, n_slc=seq // SLC_BLOCK),
        grid=(batch, nq),
        in_specs=[
            pl.BlockSpec((ATT_TQ, NSA_WIDTH), row),
            pl.BlockSpec((1,) + kp.shape[1:], per_b),
            pl.BlockSpec((1,) + vp.shape[1:], per_b),
            pl.BlockSpec(ovt.shape, lambda b, i: (0, 0)),
        ],
        out_specs=[
            pl.BlockSpec((ATT_TQ, NSA_WIDTH), row),
            pl.BlockSpec((ATT_TQ, 2 * NSA_WIDTH), row),
        ],
        out_shape=[
            jax.ShapeDtypeStruct((t, NSA_WIDTH), F32),
            jax.ShapeDtypeStruct((t, 2 * NSA_WIDTH), BF16),
        ],
        compiler_params=_params(("parallel", "parallel")),
        name="cmpattn",
    )(q, kp, vp, ovt)


def _nsa_body(qa_ref, kv_ref, oc_ref, sm_ref, out_ref, k2s, v2s, k2w, v2w):
    tq, kc = ATT_TQ, ATT_KC
    g = pl.program_id(1)
    qi = pl.program_id(2)
    n_chunks = k2s.shape[0]

    @pl.when(qi == 0)
    def _build():
        first = g == 0
        row = lax.broadcasted_iota(jnp.int32, (2 * kc, LANES), 0)
        lane = lax.broadcasted_iota(jnp.int32, (2 * kc, LANES), 1)
        for c in range(n_chunks):
            rows = slice(c * kc, (c + 1) * kc)
            for src, dst in ((0, k2s), (1, v2s), (2, k2w), (3, v2w)):
                top, bot = _pair_blockdiag(kv_ref[rows, src * LANES:(src + 1) * LANES], first)
                dst[c, 0:kc, 0:LANES] = top.astype(BF16)
                dst[c, kc:2 * kc, 0:LANES] = bot.astype(BF16)
            sel_blk = (c * kc + (row % kc)) // SLC_BLOCK
            k2s[c, :, LANES:2 * LANES] = jnp.where(lane == sel_blk, 1.0, 0.0).astype(BF16)

    r = lax.broadcasted_iota(jnp.int32, (tq, 2 * kc), 0)
    col = lax.broadcasted_iota(jnp.int32, (tq, 2 * kc), 1) % kc
    lower = jnp.where(col <= r, 0.0, NEG_INF)
    upper = jnp.where(col > r, 0.0, NEG_INF)
    lane_lo = lax.broadcasted_iota(jnp.int32, (tq, LANES), 1) < NSA_DH

    def step(q, k2, v2, bias, carry):
        ma, la, mb, lb, acc = carry
        s2 = _dot_nt(q, k2)
        if bias is not None:
            s2 = s2 + bias
        sa, sb = s2[:, :kc], s2[:, kc:]
        ma_n = jnp.maximum(ma, jnp.max(sa, axis=-1, keepdims=True))
        mb_n = jnp.maximum(mb, jnp.max(sb, axis=-1, keepdims=True))
        pa = jnp.exp(sa - ma_n)
        pb = jnp.exp(sb - mb_n)
        aa = jnp.exp(ma - ma_n)
        ab = jnp.exp(mb - mb_n)
        la = aa * la + jnp.sum(pa, axis=-1, keepdims=True)
        lb = ab * lb + jnp.sum(pb, axis=-1, keepdims=True)
        pv = _dot(jnp.concatenate([pa, pb], axis=-1).astype(BF16), v2)
        acc = jnp.where(lane_lo, aa, ab) * acc + pv
        return ma_n, la, mb_n, lb, acc

    def finish(carry):
        _, la, _, lb, acc = carry
        return acc / jnp.where(lane_lo, la, lb)

    init = (jnp.full((tq, 1), M_INIT, F32), jnp.zeros((tq, 1), F32),
            jnp.full((tq, 1), M_INIT, F32), jnp.zeros((tq, 1), F32),
            jnp.zeros((tq, LANES), F32))

    sig = jax.nn.sigmoid(sm_ref[...])
    lane_g = lax.broadcasted_iota(jnp.int32, (tq, LANES), 1)

    def gate(branch, head):
        idx = branch * NSA_HEADS + head
        return jnp.sum(jnp.where(lane_g == idx, sig, 0.0), axis=-1, keepdims=True)

    c1 = jnp.maximum(qi - 1, 0)
    c2 = jnp.maximum(qi - 2, 0)
    off1 = jnp.where(qi >= 1, 0.0, NEG_INF)
    off2 = jnp.where(qi >= 2, 0.0, NEG_INF)

    for j in range(2):
        qa = qa_ref[:, 2 * j * LANES:(2 * j + 2) * LANES]
        q2 = qa[:, :LANES]
        carry = step(qa, k2s[qi], v2s[qi], lower, init)
        carry = lax.fori_loop(0, qi, lambda c, cr: step(qa, k2s[c], v2s[c], None, cr), carry)
        o_s = finish(carry)
        carry = step(q2, k2w[qi], v2w[qi], lower, init)
        carry = step(q2, k2w[c1], v2w[c1], off1, carry)
        carry = step(q2, k2w[c2], v2w[c2], upper + off2, carry)
        o_w = finish(carry)
        ha = NSA_HEADS // NSA_GROUPS * g + 2 * j
        gc = jnp.where(lane_lo, gate(0, ha), gate(0, ha + 1))
        gs = jnp.where(lane_lo, gate(1, ha), gate(1, ha + 1))
        gw = jnp.where(lane_lo, gate(2, ha), gate(2, ha + 1))
        out_ref[:, j * LANES:(j + 1) * LANES] = (
            gc * oc_ref[:, j * LANES:(j + 1) * LANES] + gs * o_s + gw * o_w)


def _nsa(qa, kv, oc, sm, batch, seq):
    t = qa.shape[0]
    nq = seq // ATT_TQ
    n_chunks = seq // ATT_KC
    return pl.pallas_call(
        _nsa_body,
        grid=(batch, NSA_GROUPS, nq),
        in_specs=[
            pl.BlockSpec((ATT_TQ, 4 * LANES), lambda b, g, i: (b * nq + i, g)),
            pl.BlockSpec((seq, 4 * LANES), lambda b, g, i: (b, 0)),
            pl.BlockSpec((ATT_TQ, 2 * LANES), lambda b, g, i: (b * nq + i, g)),
            pl.BlockSpec((ATT_TQ, LANES), lambda b, g, i: (b * nq + i, 0)),
        ],
        out_specs=pl.BlockSpec((ATT_TQ, 2 * LANES), lambda b, g, i: (b * nq + i, g)),
        out_shape=jax.ShapeDtypeStruct((t, NSA_WIDTH), F32),
        scratch_shapes=[
            pltpu.VMEM((n_chunks, 2 * ATT_KC, 2 * LANES), BF16),
            pltpu.VMEM((n_chunks, 2 * ATT_KC, LANES), BF16),
            pltpu.VMEM((n_chunks, 2 * ATT_KC, LANES), BF16),
            pltpu.VMEM((n_chunks, 2 * ATT_KC, LANES), BF16),
        ],
        compiler_params=_params(("parallel", "parallel", "arbitrary")),
        name="nsa_attn",
    )(qa, kv, oc, sm)


def _mlstm_body(q_ref, k_ref, v_ref, o_ref, sm_ref, ir_ref, fr_ref, cwq_ref, cwk_ref,
                cbq_ref, cbk_ref, ng_ref, gb_ref, out_ref,
                pad_ref, qs_ref, ks_ref, va_ref, col_ref, row_ref, c_ref, m_ref):
    seq = q_ref.shape[0]
    L = MLSTM_CHUNK
    nc = seq // L
    h = pl.program_id(1)
    i_bias = gb_ref[0, 0:1, 0:1]
    f_bias = gb_ref[0, 1:2, 0:1]

    pad_ref[0:8, :] = jnp.zeros((8, MLSTM_DH), F32)
    for src, w_ref, b_ref, dst, scale in ((q_ref, cwq_ref, cbq_ref, qs_ref, 1.0),
                                          (k_ref, cwk_ref, cbk_ref, ks_ref, MLSTM_DH ** -0.5)):
        pad_ref[8:8 + seq, :] = src[...]
        y = b_ref[...] + w_ref[0:1, :] * pad_ref[pl.ds(5, seq), :]
        for j in range(1, MLSTM_CONV):
            y = y + w_ref[j:j + 1, :] * pad_ref[pl.ds(5 + j, seq), :]
        dst[...] = (y * jax.nn.sigmoid(y) * scale).astype(BF16)

    lane_v = lax.broadcasted_iota(jnp.int32, (seq, MLSTM_DH), 1)
    va_ref[:, 0:MLSTM_DH] = v_ref[...].astype(BF16)
    va_ref[:, MLSTM_DH:2 * MLSTM_DH] = jnp.where(lane_v == 0, 1.0, 0.0).astype(BF16)

    sm = sm_ref[...]
    i_col = jnp.sum(jnp.where(lane_v == SM_I + h, sm, 0.0), axis=-1, keepdims=True) + i_bias
    f_col = jax.nn.log_sigmoid(
        jnp.sum(jnp.where(lane_v == SM_F + h, sm, 0.0), axis=-1, keepdims=True) + f_bias)
    col_ref[...] = jnp.where(lane_v == 0, i_col, jnp.where(lane_v == 1, f_col, 0.0))

    i_row = ir_ref[0, 0] + i_bias
    f_row = jax.nn.log_sigmoid(fr_ref[0, 0] + f_bias)
    rr = lax.broadcasted_iota(jnp.int32, (L, L), 0)
    cc = lax.broadcasted_iota(jnp.int32, (L, L), 1)
    upper_tri = (rr <= cc).astype(F32)
    lower_tri = (rr >= cc).astype(F32)
    row_ref[0:nc, :] = i_row
    row_ref[nc:2 * nc, :] = _dot_exact(f_row, upper_tri)

    c_ref[...] = jnp.zeros(c_ref.shape, F32)
    m_ref[...] = jnp.zeros(m_ref.shape, F32)
    tril = rr >= cc
    ng = ng_ref[...]

    def chunk(c, _):
        rows = pl.ds(pl.multiple_of(c * L, L), L)
        qc = qs_ref[rows, :]
        kc = ks_ref[rows, :]
        vac = va_ref[rows, :]
        colc = col_ref[rows, :]
        ic_col = colc[:, 0:1]
        bcum_col = _dot_exact(lower_tri, colc)[:, 1:2]
        ic_row = row_ref[pl.ds(c, 1), :]
        bcum_row = row_ref[pl.ds(nc + c, 1), :]
        m_prev = m_ref[...]
        b_last = bcum_row[:, L - 1:L]

        d = jnp.where(tril, bcum_col - bcum_row + ic_row, NEG_INF)
        inter = bcum_col + m_prev
        m_t = jnp.maximum(jnp.max(d, axis=-1, keepdims=True), inter)
        w_in = jnp.exp(d - m_t)
        w_prev = jnp.exp(inter - m_t)
        sc = _dot_nt(qc, kc) * w_in
        cst = c_ref[...]
        tot = _dot(sc.astype(BF16), vac) + w_prev * _dot(qc, cst.astype(BF16))
        num = tot[:, 0:MLSTM_DH]
        den = tot[:, MLSTM_DH:MLSTM_DH + 1]
        hh = num / jnp.maximum(jnp.abs(den), jnp.exp(-m_t))
        hh = hh * lax.rsqrt(jnp.mean(hh * hh, axis=-1, keepdims=True) + EPS) * ng
        out_ref[rows, :] = hh * jax.nn.sigmoid(o_ref[rows, :])

        m_new = jnp.maximum(b_last + m_prev,
                            jnp.max(b_last - bcum_row + ic_row, axis=-1, keepdims=True))
        wk = jnp.exp(b_last - bcum_col + ic_col - m_new)
        kw = (wk * kc.astype(F32)).astype(BF16)
        c_ref[...] = jnp.exp(b_last + m_prev - m_new) * cst + _dot_tn(kw, vac)
        m_ref[...] = m_new
        return 0

    lax.fori_loop(0, nc, chunk, 0)


def _mlstm(m, sm, gates_rows, conv_w, conv_b, norm_g, gate_bias, batch, seq):
    t = m.shape[0]
    nh = MLSTM_HEADS
    nc = seq // MLSTM_CHUNK
    dh = MLSTM_DH
    blk = lambda part: pl.BlockSpec((seq, dh), lambda b, h: (b, part * nh + h))
    return pl.pallas_call(
        _mlstm_body,
        grid=(batch, nh),
        in_specs=[
            blk(0), blk(1), blk(2), blk(3),
            pl.BlockSpec((seq, LANES), lambda b, h: (b, 0)),
            pl.BlockSpec((1, 1, nc, MLSTM_CHUNK), lambda b, h: (b, h, 0, 0)),
            pl.BlockSpec((1, 1, nc, MLSTM_CHUNK), lambda b, h: (b, nh + h, 0, 0)),
            pl.BlockSpec((MLSTM_CONV, dh), lambda b, h: (0, h)),
            pl.BlockSpec((MLSTM_CONV, dh), lambda b, h: (0, nh + h)),
            pl.BlockSpec((1, dh), lambda b, h: (0, h)),
            pl.BlockSpec((1, dh), lambda b, h: (0, nh + h)),
            pl.BlockSpec((1, dh), lambda b, h: (0, h)),
            pl.BlockSpec((1, 8, LANES), lambda b, h: (h, 0, 0)),
        ],
        out_specs=pl.BlockSpec((seq, dh), lambda b, h: (b, h)),
        out_shape=jax.ShapeDtypeStruct((t, MLSTM_WIDTH), F32),
        scratch_shapes=[
            pltpu.VMEM((seq + 8, dh), F32),
            pltpu.VMEM((seq, dh), BF16),
            pltpu.VMEM((seq, dh), BF16),
            pltpu.VMEM((seq, 2 * dh), BF16),
            pltpu.VMEM((seq, LANES), F32),
            pltpu.VMEM((2 * nc, MLSTM_CHUNK), F32),
            pltpu.VMEM((dh, 2 * dh), F32),
            pltpu.VMEM((1, 1), F32),
        ],
        compiler_params=_params(("parallel", "arbitrary")),
        name="mlstm",
    )(m, m, m, m, sm, gates_rows, gates_rows, conv_w, conv_w, conv_b, conv_b, norm_g, gate_bias)


def _outproj_body(x_ref, ya_ref, yb_ref, g_ref, w_ref, o_ref):
    y = (_dot(ya_ref[...].astype(BF16), w_ref[0:NSA_WIDTH, :])
         + _dot(yb_ref[...].astype(BF16), w_ref[NSA_WIDTH:, :]))
    o_ref[...] = x_ref[...] + _rms(y, g_ref[...])


def _outproj(x, ya, yb, g, w):
    t = x.shape[0]
    row = lambda i: (i, 0)
    const = lambda i: (0, 0)
    return pl.pallas_call(
        _outproj_body,
        grid=(t // PROJ_TM,),
        in_specs=[
            pl.BlockSpec((PROJ_TM, D_MODEL), row),
            pl.BlockSpec((PROJ_TM, NSA_WIDTH), row),
            pl.BlockSpec((PROJ_TM, MLSTM_WIDTH), row),
            pl.BlockSpec((1, D_MODEL), const),
            pl.BlockSpec((D_MODEL, D_MODEL), const),
        ],
        out_specs=pl.BlockSpec((PROJ_TM, D_MODEL), row),
        out_shape=jax.ShapeDtypeStruct((t, D_MODEL), F32),
        compiler_params=_params(("parallel",)),
        name="outproj",
    )(x, ya, yb, g, w)


def _rope_tables(seq):
    pos = jnp.arange(seq, dtype=F32)
    inv_freq = ROPE_THETA ** (-jnp.arange(0, ROPE_DIMS, 2, dtype=F32) / ROPE_DIMS)
    ang = pos[:, None] * inv_freq[None, :]
    cos, sin = jnp.cos(ang), jnp.sin(ang)
    half = ROPE_DIMS // 2
    rest = NSA_DH - ROPE_DIMS
    ones = jnp.ones((seq, rest), F32)
    zeros = jnp.zeros((seq, rest), F32)
    zh = jnp.zeros((seq, half), F32)
    cos64 = jnp.concatenate([cos, cos, ones], axis=1)
    sup64 = jnp.concatenate([-sin, zh, zeros], axis=1)
    sdn64 = jnp.concatenate([zh, sin, zeros], axis=1)
    tile = lambda a: jnp.concatenate([a, a], axis=1)
    return tile(cos64), tile(sup64), tile(sdn64)


def _reorder_w_in(w):
    nsa_end = NSA_WIDTH + 6 * LANES
    gates_end = nsa_end + 3 * NSA_HEADS
    m_end = gates_end + 4 * MLSTM_WIDTH
    small = jnp.concatenate([w[:, nsa_end:gates_end], w[:, m_end:]], axis=1)
    small = jnp.pad(small, ((0, 0), (0, LANES - small.shape[1])))
    return jnp.concatenate([w[:, :nsa_end], w[:, gates_end:m_end], small], axis=1).astype(BF16)


def _compress_weights(pe, w1, w2):
    half = CMP_LEN // 2
    w1r = w1.reshape(2, 2, half, NSA_DH, CMP_HIDDEN)
    z = jnp.zeros_like(w1r)
    big = jnp.stack([jnp.concatenate([w1r, z], axis=-1),
                     jnp.concatenate([z, w1r], axis=-1)], axis=3)
    big = big.reshape(2, 2, half * NSA_GROUPS * NSA_DH, NSA_GROUPS * CMP_HIDDEN).astype(BF16)
    wa, wb = big[:, 0], big[:, 1]
    z2 = jnp.zeros_like(w2)
    w2big = jnp.concatenate([jnp.concatenate([w2, z2], axis=-1),
                             jnp.concatenate([z2, w2], axis=-1)], axis=1).astype(BF16)
    per = pe.reshape(2, 2, half, 1, NSA_DH)
    per = jnp.broadcast_to(per, (2, 2, half, NSA_GROUPS, NSA_DH)).reshape(2, 2, 1, -1)
    per = jnp.broadcast_to(per, (2, 2, 8, per.shape[-1])).astype(BF16)
    return wa, wb, w2big, per[:, 0], per[:, 1]


def _overlap_t(n_cmp_pad, n_slc):
    ci = np.arange(n_cmp_pad)[None, :] * CMP_STRIDE
    sj = np.arange(n_slc)[:, None] * SLC_BLOCK
    ov = ((ci < sj + SLC_BLOCK) & (ci + CMP_LEN > sj)).astype(np.float32)
    out = np.zeros((LANES, n_cmp_pad), np.float32)
    out[:n_slc] = ov
    return jnp.asarray(out)


def kernel(x, norm_g, ffn1_w_gu, ffn1_w_down, ffn2_w_gu, ffn2_w_down, mix_w_in, mix_w_out,
           nsa_cmp_pe, nsa_cmp_w1, nsa_cmp_w2, mlstm_conv_w, mlstm_conv_b, mlstm_i_bias,
           mlstm_f_bias, mlstm_norm_g):
    batch, seq, d = x.shape
    depth = norm_g.shape[0]
    t = batch * seq
    n_str = seq // CMP_STRIDE
    assert d == D_MODEL and seq % ATT_TQ == 0 and t % FFN_TM == 0 and seq % PROJ_TM == 0
    assert n_str == LANES and seq // SLC_BLOCK <= LANES

    cos_t, sup_t, sdn_t = _rope_tables(seq)
    ovt = _overlap_t(n_str, seq // SLC_BLOCK)
    xf = x.reshape(t, d)
    for l in range(depth):
        g = norm_g[l].reshape(6, 1, d)
        xf = _ffn(xf, g[0], g[1], ffn1_w_gu[l].astype(BF16), ffn1_w_down[l].astype(BF16))

        q, kc, vc, kv, m, sm = _inproj(xf, g[2], _reorder_w_in(mix_w_in[l]),
                                       cos_t, sup_t, sdn_t, seq)
        wa, wb, w2big, pea, peb = _compress_weights(nsa_cmp_pe[l], nsa_cmp_w1[l], nsa_cmp_w2[l])
        kp, vp = _compress(kc.reshape(batch, n_str, CMP_STRIDE * LANES),
                           vc.reshape(batch, n_str, CMP_STRIDE * LANES), wa, wb, w2big, pea, peb)
        oc, qa = _cmpattn(q, kp, vp, ovt, batch, seq)
        y_nsa = _nsa(qa, kv, oc, sm, batch, seq)

        gates_rows = sm[:, SM_I:SM_I + 2 * MLSTM_HEADS].reshape(batch, seq, 2 * MLSTM_HEADS)
        gates_rows = gates_rows.transpose(0, 2, 1).reshape(
            batch, 2 * MLSTM_HEADS, seq // MLSTM_CHUNK, MLSTM_CHUNK)
        gate_bias = jnp.zeros((MLSTM_HEADS, 8, LANES), F32)
        gate_bias = gate_bias.at[:, 0, :].set(mlstm_i_bias[l][:, None])
        gate_bias = gate_bias.at[:, 1, :].set(mlstm_f_bias[l][:, None])
        y_m = _mlstm(m, sm, gates_rows, mlstm_conv_w[l], mlstm_conv_b[l].reshape(1, -1),
                     mlstm_norm_g[l].reshape(1, -1), gate_bias, batch, seq)

        xf = _outproj(xf, y_nsa, y_m, g[3], mix_w_out[l].astype(BF16))
        xf = _ffn(xf, g[4], g[5], ffn2_w_gu[l].astype(BF16), ffn2_w_down[l].astype(BF16))
    return xf.reshape(batch, seq, d)
```
